```python
import jax, jax.numpy as jnp
from jax import lax
import numpy as np

D_MODEL = 1024
BATCH = 2
SEQ = 8192
DEPTH = 1

HEAD_DIM = 64
N_HEADS_DIL = 8
N_HEADS_NA = 8
D_DIL = N_HEADS_DIL * HEAD_DIM
D_NA = N_HEADS_NA * HEAD_DIM
D_MIX = D_DIL + D_NA
DIL_PATTERNS = ((128, 1), (512, 4), (2048, 16))
DIL_BLOCK = 64
GRID_W = 64
NA_KH = 8
NA_KW = 16
NA_QCB = NA_KW
NA_KCB = 2 * NA_KW
D_FF = 2816
N_MOD = 9
DEEPNORM_ALPHA = (2.0 * DEPTH) ** 0.25
DEEPNORM_BETA = (8.0 * DEPTH) ** -0.25
LN_EPS = 1e-5
NEG_INF = -1e30

kernel_name = "hybrid_dilated_neighbourhood_macaron_encoder"


def _layer_norm(x):
    xf = x.astype(jnp.float32)
    mu = jnp.mean(xf, axis=-1, keepdims=True)
    var = jnp.mean(jnp.square(xf - mu), axis=-1, keepdims=True)
    return (xf - mu) * lax.rsqrt(var + LN_EPS)


def _swiglu(u, w_gate, w_up, w_down):
    return (jax.nn.silu(u @ w_gate) * (u @ w_up)) @ w_down


def _alibi_slopes(n):
    return 2.0 ** (-8.0 * jnp.arange(1, n + 1, dtype=jnp.float32) / n)


def _dilated_branch(q, k, v, slopes, window, dilation):
    B, S, H, Dh = q.shape
    QB = DIL_BLOCK
    radius = window // (2 * dilation)
    L = -(-S // dilation)
    Lp = -(-L // QB) * QB
    nb = Lp // QB
    pad = Lp * dilation - S

    def to_sub(a):
        a = jnp.pad(a, ((0, 0), (0, pad), (0, 0), (0, 0)))
        return a.reshape(B, Lp, dilation, H, Dh).transpose(0, 2, 1, 3, 4)

    def band(a):
        a = jnp.pad(a, ((0, 0), (0, 0), (QB, QB), (0, 0), (0, 0)))
        a = a.reshape(B, dilation, nb + 2, QB, H, Dh)
        return jnp.concatenate([a[:, :, :-2], a[:, :, 1:-1], a[:, :, 2:]], axis=3)

    qb = to_sub(q).reshape(B, dilation, nb, QB, H, Dh)
    kw = band(to_sub(k))
    vw = band(to_sub(v))

    s = jnp.einsum('brinhe,brimhe->brihnm', qb, kw).astype(jnp.float32) * (Dh ** -0.5)
    n_idx = jnp.arange(QB)
    m_idx = jnp.arange(3 * QB)
    rel = m_idx[None, :] - QB - n_idx[:, None]
    jk = (jnp.arange(nb)[:, None] - 1) * QB + m_idx[None, :]
    tk = jk[None] * dilation + jnp.arange(dilation)[:, None, None]
    key_ok = (jk[None] >= 0) & (tk < S)
    valid = (jnp.abs(rel) <= radius)[None, None] & key_ok[:, :, None, :]
    bias = -slopes[:, None, None] * (dilation * jnp.abs(rel)).astype(jnp.float32)
    s = jnp.where(valid[None, :, :, None], s + bias, NEG_INF)
    mx = jnp.max(s, axis=-1, keepdims=True)
    p = jnp.exp(s - mx)
    l = jnp.sum(p, axis=-1, keepdims=True)
    o = jnp.einsum('brihnm,brimhe->brinhe', p.astype(v.dtype), vw).astype(jnp.float32)
    o = o / jnp.swapaxes(l, 3, 4)
    lse = jnp.swapaxes((mx + jnp.log(l))[..., 0], 3, 4)

    def from_sub(a):
        tail = a.shape[4:]
        a = a.reshape((B, dilation, Lp) + tail)
        a = jnp.moveaxis(a, 1, 2).reshape((B, Lp * dilation) + tail)
        return a[:, :S]

    return from_sub(o), from_sub(lse)


def _dilated_attention(q, k, v):
    slopes = _alibi_slopes(q.shape[2])
    outs, lses = [], []
    for window, dilation in DIL_PATTERNS:
        o, lse = _dilated_branch(q, k, v, slopes, window, dilation)
        outs.append(o)
        lses.append(lse)
    w = jax.nn.softmax(jnp.stack(lses, axis=0), axis=0)
    return jnp.einsum('pbsh,pbshe->bshe', w, jnp.stack(outs, axis=0))


def _neighbourhood_attention(q, k, v, rpb):
    B, S, H, Dh = q.shape
    rows = S // GRID_W
    kh = min(NA_KH, rows)
    qg = q.reshape(B, rows, GRID_W, H, Dh)
    kg = k.reshape(B, rows, GRID_W, H, Dh)
    vg = v.reshape(B, rows, GRID_W, H, Dh)
    r = jnp.arange(rows)
    row_start = jnp.clip(r - kh // 2, 0, rows - kh)
    row_idx = row_start[:, None] + jnp.arange(kh)[None, :]
    kr = kg[:, row_idx]
    vr = vg[:, row_idx]
    dr = row_idx - r[:, None]
    bias_rows = rpb[:, dr + NA_KH - 1]
    outs = []
    for cb in range(GRID_W // NA_QCB):
        c0 = cb * NA_QCB
        kc0 = min(max(c0 - NA_KW // 2, 0), GRID_W - NA_KCB)
        qcols = c0 + jnp.arange(NA_QCB)
        kcols = kc0 + jnp.arange(NA_KCB)
        col_start = jnp.clip(qcols - NA_KW // 2, 0, GRID_W - NA_KW)
        colmask = (kcols[None, :] >= col_start[:, None]) & (kcols[None, :] < col_start[:, None] + NA_KW)
        dc_idx = jnp.clip(kcols[None, :] - qcols[:, None] + NA_KW - 1, 0, 2 * NA_KW - 2)
        bias = bias_rows[..., dc_idx].transpose(1, 0, 3, 2, 4)
        qblk = qg[:, :, c0:c0 + NA_QCB]
        kblk = kr[:, :, :, kc0:kc0 + NA_KCB]
        vblk = vr[:, :, :, kc0:kc0 + NA_KCB]
        s = jnp.einsum('brqhe,brkche->brhqkc', qblk, kblk).astype(jnp.float32) * (Dh ** -0.5)
        s = jnp.where(colmask[:, None, :], s + bias, NEG_INF)
        p = jax.nn.softmax(s, axis=(-2, -1))
        outs.append(jnp.einsum('brhqkc,brkche->brqhe', p.astype(v.dtype), vblk))
    return jnp.concatenate(outs, axis=2).reshape(B, S, H, Dh)


def _mixer(u, w_in, rpb, w_out):
    B, S, _ = u.shape
    proj = u @ w_in
    cuts = [D_DIL, 2 * D_DIL, 3 * D_DIL, 3 * D_DIL + D_NA, 3 * D_DIL + 2 * D_NA]
    qa, ka, va, qb, kb, vb = jnp.split(proj, cuts, axis=-1)
    heads_a = lambda t: t.reshape(B, S, N_HEADS_DIL, HEAD_DIM)
    heads_b = lambda t: t.reshape(B, S, N_HEADS_NA, HEAD_DIM)
    ya = _dilated_attention(heads_a(qa), heads_a(ka), heads_a(va)).astype(u.dtype).reshape(B, S, D_DIL)
    yb = _neighbourhood_attention(heads_b(qb), heads_b(kb), heads_b(vb), rpb).astype(u.dtype).reshape(B, S, D_NA)
    return jnp.concatenate([ya, yb], axis=-1) @ w_out


def setup_inputs(seed: int = 0) -> dict:
    key = jax.random.key(seed)
    ks = jax.random.split(key, 20)
    f32 = jnp.float32
    L, D = DEPTH, D_MODEL

    def nrm(k, shape, scale):
        return jax.random.normal(k, shape, f32) * scale

    x = nrm(ks[0], (BATCH, SEQ, D), 1.0)
    c = nrm(ks[1], (BATCH, D), 1.0)
    w_ada = nrm(ks[2], (L, D, N_MOD * D), D ** -0.5)
    b_ada = nrm(ks[3], (L, N_MOD * D), 0.02)
    ffn1_w_gate = nrm(ks[4], (L, D, D_FF), D ** -0.5 * DEEPNORM_BETA)
    ffn1_w_up = nrm(ks[5], (L, D, D_FF), D ** -0.5 * DEEPNORM_BETA)
    ffn1_w_down = nrm(ks[6], (L, D_FF, D), D_FF ** -0.5 * DEEPNORM_BETA)
    qk_a = nrm(ks[7], (L, D, 2 * D_DIL), D ** -0.5)
    v_a = nrm(ks[8], (L, D, D_DIL), D ** -0.5 * DEEPNORM_BETA)
    qk_b = nrm(ks[9], (L, D, 2 * D_NA), D ** -0.5)
    v_b = nrm(ks[10], (L, D, D_NA), D ** -0.5 * DEEPNORM_BETA)
    w_in = jnp.concatenate([qk_a, v_a, qk_b, v_b], axis=-1)
    rpb = nrm(ks[11], (L, N_HEADS_NA, 2 * NA_KH - 1, 2 * NA_KW - 1), 0.1)
    w_out = nrm(ks[12], (L, D_MIX, D), D_MIX ** -0.5 * DEEPNORM_BETA)
    ffn2_w_gate = nrm(ks[13], (L, D, D_FF), D ** -0.5 * DEEPNORM_BETA)
    ffn2_w_up = nrm(ks[14], (L, D, D_FF), D ** -0.5 * DEEPNORM_BETA)
    ffn2_w_down = nrm(ks[15], (L, D_FF, D), D_FF ** -0.5 * DEEPNORM_BETA)
    ln_g = 1.0 + nrm(ks[16], (L, 3, D), 0.02)
    ln_b = nrm(ks[17], (L, 3, D), 0.02)
    return {"x": x, "c": c, "w_ada": w_ada, "b_ada": b_ada,
            "ffn1_w_gate": ffn1_w_gate, "ffn1_w_up": ffn1_w_up, "ffn1_w_down": ffn1_w_down,
            "w_in": w_in, "rpb": rpb, "w_out": w_out,
            "ffn2_w_gate": ffn2_w_gate, "ffn2_w_up": ffn2_w_up, "ffn2_w_down": ffn2_w_down,
            "ln_g": ln_g, "ln_b": ln_b}


def reference(x, c, w_ada, b_ada, ffn1_w_gate, ffn1_w_up, ffn1_w_down, w_in, rpb, w_out,
              ffn2_w_gate, ffn2_w_up, ffn2_w_down, ln_g, ln_b):
    h = x
    for layer in range(DEPTH):
        mods = (jax.nn.silu(c) @ w_ada[layer] + b_ada[layer]).reshape(c.shape[0], N_MOD, D_MODEL)

        def sublayer(h, j, fn, resid_w):
            shift = mods[:, 3 * j][:, None]
            scale = mods[:, 3 * j + 1][:, None]
            gate = mods[:, 3 * j + 2][:, None]
            u = (_layer_norm(h) * (1.0 + scale) + shift).astype(h.dtype)
            z = DEEPNORM_ALPHA * h.astype(jnp.float32) + resid_w * gate * fn(u)
            return (_layer_norm(z) * ln_g[layer, j] + ln_b[layer, j]).astype(h.dtype)

        h = sublayer(h, 0, lambda u: _swiglu(u, ffn1_w_gate[layer], ffn1_w_up[layer], ffn1_w_down[layer]), 0.5)
        h = sublayer(h, 1, lambda u: _mixer(u, w_in[layer], rpb[layer], w_out[layer]), 1.0)
        h = sublayer(h, 2, lambda u: _swiglu(u, ffn2_w_gate[layer], ffn2_w_up[layer], ffn2_w_down[layer]), 0.5)
    return h
```

```python
import functools

import numpy as np
import jax
import jax.numpy as jnp
from jax import lax
from jax.experimental import pallas as pl
from jax.experimental.pallas import tpu as pltpu

D_MODEL = 1024
HEAD_DIM = 64
N_HEADS = 8
D_GROUP = N_HEADS * HEAD_DIM
DIL_PATTERNS = ((128, 1), (512, 4), (2048, 16))
DIL_RADIUS = 64
GRID_W = 64
NA_KH = 8
NA_KW = 16
D_FF = 2816
N_MOD = 9
DEPTH = 1
DEEPNORM_ALPHA = (2.0 * DEPTH) ** 0.25
LN_EPS = 1e-5
NEG_INF = -1e30

LANES = 128
VMEM_LIMIT = 56 * 1024 * 1024

TM_FFN = 1024
TF_FFN = 256
TM_PROJ = 512
TM_OUT = 512
TN_MODS = 1024
DIL_SB = 2048
DIL_QB = 128
DIL_KB = 256


def _layer_norm(x):
    mu = jnp.mean(x, axis=-1, keepdims=True)
    xc = x - mu
    var = jnp.mean(xc * xc, axis=-1, keepdims=True)
    return xc * lax.rsqrt(var + LN_EPS)


def _modulate(h, mods, j):
    shift = mods[3 * j:3 * j + 1]
    scale = mods[3 * j + 1:3 * j + 2]
    return _layer_norm(h) * (1.0 + scale) + shift


def _residual_norm(h, branch, mods, lng, lnb, j, resid_w):
    gate = mods[3 * j + 2:3 * j + 3]
    z = DEEPNORM_ALPHA * h + resid_w * gate * branch
    return _layer_norm(z) * lng[j:j + 1] + lnb[j:j + 1]


def _mods_kernel(ct_ref, w_ref, b_ref, o_ref):
    ct = ct_ref[...]
    a = ct * jax.nn.sigmoid(ct)
    w = w_ref[...]
    rows = [jnp.sum(w * a[:, b:b + 1], axis=0, keepdims=True) for b in range(ct.shape[1])]
    o_ref[...] = jnp.concatenate(rows, axis=0) + b_ref[...]


def _mods(c, w_ada, b_ada):
    B, D = c.shape
    n = w_ada.shape[1]
    return pl.pallas_call(
        _mods_kernel,
        out_shape=jax.ShapeDtypeStruct((B, n), jnp.float32),
        grid=(n // TN_MODS,),
        in_specs=[pl.BlockSpec((D, B), lambda i: (0, 0)),
                  pl.BlockSpec((D, TN_MODS), lambda i: (0, i)),
                  pl.BlockSpec((1, TN_MODS), lambda i: (0, i))],
        out_specs=pl.BlockSpec((B, TN_MODS), lambda i: (0, i)),
        compiler_params=pltpu.CompilerParams(dimension_semantics=("arbitrary",),
                                             vmem_limit_bytes=VMEM_LIMIT),
        name="mods",
    )(c.T, w_ada, b_ada.reshape(1, n))


def _ffn_kernel(j, h_ref, mods_ref, wg_ref, wu_ref, wd_ref, lng_ref, lnb_ref, o_ref, u_sc, acc_sc):
    f = pl.program_id(1)

    @pl.when(f == 0)
    def _():
        u_sc[...] = _modulate(h_ref[...], mods_ref[0], j).astype(jnp.bfloat16)
        acc_sc[...] = jnp.zeros_like(acc_sc)

    u = u_sc[...]
    g = jnp.dot(u, wg_ref[...], preferred_element_type=jnp.float32)
    up = jnp.dot(u, wu_ref[...], preferred_element_type=jnp.float32)
    a = (g * jax.nn.sigmoid(g) * up).astype(jnp.bfloat16)
    acc_sc[...] += jnp.dot(a, wd_ref[...], preferred_element_type=jnp.float32)

    @pl.when(f == pl.num_programs(1) - 1)
    def _():
        o_ref[...] = _residual_norm(h_ref[...], acc_sc[...], mods_ref[0],
                                    lng_ref[...], lnb_ref[...], j, 0.5)


def _ffn(h, mods, wg, wu, wd, lng, lnb, j, seq):
    n, d = h.shape
    tiles_per_batch = seq // TM_FFN
    return pl.pallas_call(
        functools.partial(_ffn_kernel, j),
        out_shape=jax.ShapeDtypeStruct((n, d), jnp.float32),
        grid=(n // TM_FFN, D_FF // TF_FFN),
        in_specs=[pl.BlockSpec((TM_FFN, d), lambda i, f: (i, 0)),
                  pl.BlockSpec((1, N_MOD, d), lambda i, f: (i // tiles_per_batch, 0, 0)),
                  pl.BlockSpec((d, TF_FFN), lambda i, f: (0, f)),
                  pl.BlockSpec((d, TF_FFN), lambda i, f: (0, f)),
                  pl.BlockSpec((TF_FFN, d), lambda i, f: (f, 0)),
                  pl.BlockSpec((3, d), lambda i, f: (0, 0)),
                  pl.BlockSpec((3, d), lambda i, f: (0, 0))],
        out_specs=pl.BlockSpec((TM_FFN, d), lambda i, f: (i, 0)),
        scratch_shapes=[pltpu.VMEM((TM_FFN, d), jnp.bfloat16),
                        pltpu.VMEM((TM_FFN, d), jnp.float32)],
        compiler_params=pltpu.CompilerParams(dimension_semantics=("arbitrary", "arbitrary"),
                                             vmem_limit_bytes=VMEM_LIMIT),
        name=f"ffn{j}",
    )(h, mods, wg, wu, wd, lng, lnb)


def _proj_kernel(h_ref, mods_ref, w_ref, qa_ref, qb_ref):
    u = _modulate(h_ref[...], mods_ref[0], 1).astype(jnp.bfloat16)
    scale = HEAD_DIM ** -0.5
    for grp, out_ref in enumerate((qa_ref, qb_ref)):
        for part in range(3):
            c0 = (3 * grp + part) * D_GROUP
            y = jnp.dot(u, w_ref[:, c0:c0 + D_GROUP], preferred_element_type=jnp.float32)
            if part == 0:
                y = y * scale
            out_ref[:, part * D_GROUP:(part + 1) * D_GROUP] = y.astype(out_ref.dtype)


def _proj(h, mods, w_in, seq):
    n, d = h.shape
    tiles_per_batch = seq // TM_PROJ
    return pl.pallas_call(
        _proj_kernel,
        out_shape=(jax.ShapeDtypeStruct((n, 3 * D_GROUP), jnp.float32),
                   jax.ShapeDtypeStruct((n, 3 * D_GROUP), jnp.bfloat16)),
        grid=(n // TM_PROJ,),
        in_specs=[pl.BlockSpec((TM_PROJ, d), lambda i: (i, 0)),
                  pl.BlockSpec((1, N_MOD, d), lambda i: (i // tiles_per_batch, 0, 0)),
                  pl.BlockSpec((d, 6 * D_GROUP), lambda i: (0, 0))],
        out_specs=(pl.BlockSpec((TM_PROJ, 3 * D_GROUP), lambda i: (i, 0)),
                   pl.BlockSpec((TM_PROJ, 3 * D_GROUP), lambda i: (i, 0))),
        compiler_params=pltpu.CompilerParams(dimension_semantics=("arbitrary",),
                                             vmem_limit_bytes=VMEM_LIMIT),
        name="proj",
    )(h, mods, w_in)


def _head_masks():
    lane = lax.broadcasted_iota(jnp.int32, (1, LANES), 1)
    return lane < HEAD_DIM, lane >= HEAD_DIM


def _two_head_block(q, k, v, bias_fn):
    masks = _head_masks()
    zero = jnp.zeros((), q.dtype)
    ms, ls, acc = [], [], None
    for e in range(2):
        qe = jnp.where(masks[e], q, zero)
        s = lax.dot_general(qe, k, (((1,), (1,)), ((), ())), preferred_element_type=jnp.float32)
        s = bias_fn(e, s)
        m = jnp.max(s, axis=-1, keepdims=True)
        p = jnp.exp(s - m)
        ls.append(jnp.sum(p, axis=-1, keepdims=True))
        ms.append(m)
        ve = jnp.where(masks[e], v, zero)
        pv = jnp.dot(p.astype(v.dtype), ve, preferred_element_type=jnp.float32)
        acc = pv if acc is None else acc + pv
    return ms, ls, acc


def _dil_kernel(slopes_ref, q_ref, k_ref, v_ref, o_ref, m_sc, l_sc, acc_sc):
    hp = pl.program_id(1)
    sb = pl.program_id(2)
    seq = q_ref.shape[1]
    h0, _ = _head_masks()
    slope = (slopes_ref[2 * hp], slopes_ref[2 * hp + 1])
    row = lax.broadcasted_iota(jnp.int32, (DIL_QB, DIL_KB), 0)
    col = lax.broadcasted_iota(jnp.int32, (DIL_QB, DIL_KB), 1)

    def rows(start, size, d):
        return pl.ds(start, size) if d == 1 else pl.ds(start, size, stride=d)

    for _, d in DIL_PATTERNS:
        sub_len = seq // d
        sub_sb = DIL_SB // d
        blocks_per_sub = sub_sb // DIL_QB

        def body(i, carry, d=d, sub_len=sub_len, sub_sb=sub_sb, blocks_per_sub=blocks_per_sub):
            r = i // blocks_per_sub
            jq_rel = (i % blocks_per_sub) * DIL_QB
            jq0 = sb * sub_sb + jq_rel
            ks = jnp.clip(jq0 - DIL_RADIUS, 0, sub_len - DIL_KB)
            q = q_ref[0, rows(r + d * jq0, DIL_QB, d), :].astype(jnp.bfloat16)
            k = k_ref[0, rows(r + d * ks, DIL_KB, d), :].astype(jnp.bfloat16)
            v = v_ref[0, rows(r + d * ks, DIL_KB, d), :].astype(jnp.bfloat16)
            absrel = jnp.abs(col - row + (ks - jq0))
            valid = absrel <= DIL_RADIUS
            dist = (d * absrel).astype(jnp.float32)

            def bias_fn(e, s):
                return jnp.where(valid, s - slope[e] * dist, NEG_INF)

            ms, ls, acc = _two_head_block(q, k, v, bias_fn)
            m_blk = jnp.where(h0, ms[0], ms[1])
            l_blk = jnp.where(h0, ls[0], ls[1])
            st = rows(r + d * jq_rel, DIL_QB, d)
            if d == 1:
                m_sc[st, :] = m_blk
                l_sc[st, :] = l_blk
                acc_sc[st, :] = acc
            else:
                m_old = m_sc[st, :]
                m_new = jnp.maximum(m_old, m_blk)
                a_old = jnp.exp(m_old - m_new)
                a_blk = jnp.exp(m_blk - m_new)
                m_sc[st, :] = m_new
                l_sc[st, :] = a_old * l_sc[st, :] + a_blk * l_blk
                acc_sc[st, :] = a_old * acc_sc[st, :] + a_blk * acc
            return carry

        lax.fori_loop(0, DIL_SB // DIL_QB, body, 0)

    o_ref[0] = (acc_sc[...] / l_sc[...]).astype(o_ref.dtype)


def _dilated(qkv_a, slopes):
    B, S, _ = qkv_a.shape
    n_hp = D_GROUP // LANES
    blk = lambda part: pl.BlockSpec((1, S, LANES), lambda b, hp, sb: (b, 0, part * n_hp + hp))
    return pl.pallas_call(
        _dil_kernel,
        out_shape=jax.ShapeDtypeStruct((B, S, D_GROUP), jnp.bfloat16),
        grid=(B, n_hp, S // DIL_SB),
        in_specs=[pl.BlockSpec(memory_space=pltpu.SMEM), blk(0), blk(1), blk(2)],
        out_specs=pl.BlockSpec((1, DIL_SB, LANES), lambda b, hp, sb: (b, sb, hp)),
        scratch_shapes=[pltpu.VMEM((DIL_SB, LANES), jnp.float32)] * 3,
        compiler_params=pltpu.CompilerParams(
            dimension_semantics=("arbitrary", "arbitrary", "arbitrary"),
            vmem_limit_bytes=VMEM_LIMIT),
        name="dilated",
    )(slopes, qkv_a, qkv_a, qkv_a)


def _na_kernel(q_ref, k_ref, v_ref, bias_ref, o_ref):
    n_rows = q_ref.shape[1] // GRID_W
    h0, _ = _head_masks()
    win = NA_KH * GRID_W

    def body(r, carry):
        rs = jnp.clip(r - NA_KH // 2, 0, n_rows - NA_KH)
        delta = r - rs
        q0 = pl.multiple_of(r * GRID_W, GRID_W)
        k0 = pl.multiple_of(rs * GRID_W, GRID_W)
        q = q_ref[0, pl.ds(q0, GRID_W), :]
        k = k_ref[0, pl.ds(k0, win), :]
        v = v_ref[0, pl.ds(k0, win), :]

        def bias_fn(e, s):
            return s + bias_ref[0, e, delta]

        _, ls, acc = _two_head_block(q, k, v, bias_fn)
        o_ref[0, pl.ds(q0, GRID_W), :] = (acc / jnp.where(h0, ls[0], ls[1])).astype(o_ref.dtype)
        return carry

    lax.fori_loop(0, n_rows, body, 0)


def _na_bias_table(rpb):
    delta = np.arange(NA_KH)[:, None]
    i = np.arange(NA_KH)[None, :]
    row_idx = i - delta + NA_KH - 1
    c = np.arange(GRID_W)[:, None]
    kc = np.arange(GRID_W)[None, :]
    col_start = np.clip(c - NA_KW // 2, 0, GRID_W - NA_KW)
    valid = (kc >= col_start) & (kc < col_start + NA_KW)
    col_idx = np.clip(kc - c + NA_KW - 1, 0, 2 * NA_KW - 2)
    tab = rpb[:, row_idx[:, None, :, None], col_idx[None, :, None, :]]
    tab = jnp.where(valid[None, None, :, None, :], tab, NEG_INF)
    H = rpb.shape[0]
    return tab.reshape(H // 2, 2, NA_KH, GRID_W, NA_KH * GRID_W)


def _neighbourhood(qkv_b, bias_tab):
    B, S, _ = qkv_b.shape
    n_hp = D_GROUP // LANES
    blk = lambda part: pl.BlockSpec((1, S, LANES), lambda b, hp: (b, 0, part * n_hp + hp))
    return pl.pallas_call(
        _na_kernel,
        out_shape=jax.ShapeDtypeStruct((B, S, D_GROUP), jnp.bfloat16),
        grid=(B, n_hp),
        in_specs=[blk(0), blk(1), blk(2),
                  pl.BlockSpec((1,) + bias_tab.shape[1:], lambda b, hp: (hp, 0, 0, 0, 0))],
        out_specs=pl.BlockSpec((1, S, LANES), lambda b, hp: (b, 0, hp)),
        compiler_params=pltpu.CompilerParams(dimension_semantics=("arbitrary", "arbitrary"),
                                             vmem_limit_bytes=VMEM_LIMIT),
        name="nbr",
    )(qkv_b, qkv_b, qkv_b, bias_tab)


def _out_kernel(h_ref, ya_ref, yb_ref, mods_ref, w_ref, lng_ref, lnb_ref, o_ref):
    y = jnp.dot(ya_ref[...], w_ref[:D_GROUP, :], preferred_element_type=jnp.float32)
    y = y + jnp.dot(yb_ref[...], w_ref[D_GROUP:, :], preferred_element_type=jnp.float32)
    o_ref[...] = _residual_norm(h_ref[...], y, mods_ref[0], lng_ref[...], lnb_ref[...], 1, 1.0)


def _out_proj(h, ya, yb, mods, w_out, lng, lnb, seq):
    n, d = h.shape
    tiles_per_batch = seq // TM_OUT
    return pl.pallas_call(
        _out_kernel,
        out_shape=jax.ShapeDtypeStruct((n, d), jnp.float32),
        grid=(n // TM_OUT,),
        in_specs=[pl.BlockSpec((TM_OUT, d), lambda i: (i, 0)),
                  pl.BlockSpec((TM_OUT, D_GROUP), lambda i: (i, 0)),
                  pl.BlockSpec((TM_OUT, D_GROUP), lambda i: (i, 0)),
                  pl.BlockSpec((1, N_MOD, d), lambda i: (i // tiles_per_batch, 0, 0)),
                  pl.BlockSpec((2 * D_GROUP, d), lambda i: (0, 0)),
                  pl.BlockSpec((3, d), lambda i: (0, 0)),
                  pl.BlockSpec((3, d), lambda i: (0, 0))],
        out_specs=pl.BlockSpec((TM_OUT, d), lambda i: (i, 0)),
        compiler_params=pltpu.CompilerParams(dimension_semantics=("arbitrary",),
                                             vmem_limit_bytes=VMEM_LIMIT),
        name="out_proj",
    )(h, ya, yb, mods, w_out, lng, lnb)


def kernel(x, c, w_ada, b_ada, ffn1_w_gate, ffn1_w_up, ffn1_w_down, w_in, rpb, w_out,
           ffn2_w_gate, ffn2_w_up, ffn2_w_down, ln_g, ln_b):
    B, S, D = x.shape
    bf = jnp.bfloat16
    slopes = jnp.asarray(2.0 ** (-8.0 * np.arange(1, N_HEADS + 1) / N_HEADS), jnp.float32)
    h = x.reshape(B * S, D)
    for layer in range(DEPTH):
        mods = _mods(c, w_ada[layer], b_ada[layer]).reshape(B, N_MOD, D)
        lng, lnb = ln_g[layer], ln_b[layer]
        h = _ffn(h, mods, ffn1_w_gate[layer].astype(bf), ffn1_w_up[layer].astype(bf),
                 ffn1_w_down[layer].astype(bf), lng, lnb, 0, S)
        qkv_a, qkv_b = _proj(h, mods, w_in[layer].astype(bf), S)
        ya = _dilated(qkv_a.reshape(B, S, 3 * D_GROUP), slopes)
        yb = _neighbourhood(qkv_b.reshape(B, S, 3 * D_GROUP), _na_bias_table(rpb[layer]))
        h = _out_proj(h, ya.reshape(B * S, D_GROUP), yb.reshape(B * S, D_GROUP), mods,
                      w_out[layer].astype(bf), lng, lnb, S)
        h = _ffn(h, mods, ffn2_w_gate[layer].astype(bf), ffn2_w_up[layer].astype(bf),
                 ffn2_w_down[layer].astype(bf), lng, lnb, 2, S)
    return h.reshape(B, S, D)
```

```python
import functools

import numpy as np
import jax
import jax.numpy as jnp
from jax import lax
from jax.experimental import pallas as pl
from jax.experimental.pallas import tpu as pltpu

D_MODEL = 1024
HEAD_DIM = 64
N_HEADS = 8
D_GROUP = N_HEADS * HEAD_DIM
DIL_PATTERNS = ((128, 1), (512, 4), (2048, 16))
DIL_RADIUS = 64
GRID_W = 64
NA_KH = 8
NA_KW = 16
D_FF = 2816
N_MOD = 9
DEPTH = 1
DEEPNORM_ALPHA = (2.0 * DEPTH) ** 0.25
LN_EPS = 1e-5
NEG_INF = -1e30

LANES = 128
VMEM_LIMIT = 56 * 1024 * 1024

TM_FFN = 1024
TF_FFN = 256
TM_PROJ = 512
TM_OUT = 512
TN_MODS = 1024
DIL_SB = 2048
DIL_QB = 128
DIL_KB = 256
DIL_GROUP = 4
NA_GROUP = 4


def _layer_norm(x):
    mu = jnp.mean(x, axis=-1, keepdims=True)
    xc = x - mu
    var = jnp.mean(xc * xc, axis=-1, keepdims=True)
    return xc * lax.rsqrt(var + LN_EPS)


def _modulate(h, mods, j):
    shift = mods[3 * j:3 * j + 1]
    scale = mods[3 * j + 1:3 * j + 2]
    return _layer_norm(h) * (1.0 + scale) + shift


def _residual_norm(h, branch, mods, lng, lnb, j, resid_w):
    gate = mods[3 * j + 2:3 * j + 3]
    z = DEEPNORM_ALPHA * h + resid_w * gate * branch
    return _layer_norm(z) * lng[j:j + 1] + lnb[j:j + 1]


def _mods_kernel(ct_ref, w_ref, b_ref, o_ref):
    ct = ct_ref[...]
    a = ct * jax.nn.sigmoid(ct)
    w = w_ref[...]
    rows = [jnp.sum(w * a[:, b:b + 1], axis=0, keepdims=True) for b in range(ct.shape[1])]
    o_ref[...] = jnp.concatenate(rows, axis=0) + b_ref[...]


def _mods(c, w_ada, b_ada):
    B, D = c.shape
    n = w_ada.shape[1]
    return pl.pallas_call(
        _mods_kernel,
        out_shape=jax.ShapeDtypeStruct((B, n), jnp.float32),
        grid=(n // TN_MODS,),
        in_specs=[pl.BlockSpec((D, B), lambda i: (0, 0)),
                  pl.BlockSpec((D, TN_MODS), lambda i: (0, i)),
                  pl.BlockSpec((1, TN_MODS), lambda i: (0, i))],
        out_specs=pl.BlockSpec((B, TN_MODS), lambda i: (0, i)),
        compiler_params=pltpu.CompilerParams(dimension_semantics=("arbitrary",),
                                             vmem_limit_bytes=VMEM_LIMIT),
        name="mods",
    )(c.T, w_ada, b_ada.reshape(1, n))


def _ffn_kernel(j, h_ref, mods_ref, wg_ref, wu_ref, wd_ref, lng_ref, lnb_ref, o_ref, u_sc, acc_sc):
    f = pl.program_id(1)

    @pl.when(f == 0)
    def _():
        u_sc[...] = _modulate(h_ref[...], mods_ref[0], j).astype(jnp.bfloat16)
        acc_sc[...] = jnp.zeros_like(acc_sc)

    u = u_sc[...]
    g = jnp.dot(u, wg_ref[...], preferred_element_type=jnp.float32)
    up = jnp.dot(u, wu_ref[...], preferred_element_type=jnp.float32)
    a = (g * jax.nn.sigmoid(g) * up).astype(jnp.bfloat16)
    acc_sc[...] += jnp.dot(a, wd_ref[...], preferred_element_type=jnp.float32)

    @pl.when(f == pl.num_programs(1) - 1)
    def _():
        o_ref[...] = _residual_norm(h_ref[...], acc_sc[...], mods_ref[0],
                                    lng_ref[...], lnb_ref[...], j, 0.5)


def _ffn(h, mods, wg, wu, wd, lng, lnb, j, seq):
    n, d = h.shape
    tiles_per_batch = seq // TM_FFN
    return pl.pallas_call(
        functools.partial(_ffn_kernel, j),
        out_shape=jax.ShapeDtypeStruct((n, d), jnp.float32),
        grid=(n // TM_FFN, D_FF // TF_FFN),
        in_specs=[pl.BlockSpec((TM_FFN, d), lambda i, f: (i, 0)),
                  pl.BlockSpec((1, N_MOD, d), lambda i, f: (i // tiles_per_batch, 0, 0)),
                  pl.BlockSpec((d, TF_FFN), lambda i, f: (0, f)),
                  pl.BlockSpec((d, TF_FFN), lambda i, f: (0, f)),
                  pl.BlockSpec((TF_FFN, d), lambda i, f: (f, 0)),
                  pl.BlockSpec((3, d), lambda i, f: (0, 0)),
                  pl.BlockSpec((3, d), lambda i, f: (0, 0))],
        out_specs=pl.BlockSpec((TM_FFN, d), lambda i, f: (i, 0)),
        scratch_shapes=[pltpu.VMEM((TM_FFN, d), jnp.bfloat16),
                        pltpu.VMEM((TM_FFN, d), jnp.float32)],
        compiler_params=pltpu.CompilerParams(dimension_semantics=("arbitrary", "arbitrary"),
                                             vmem_limit_bytes=VMEM_LIMIT),
        name=f"ffn{j}",
    )(h, mods, wg, wu, wd, lng, lnb)


def _proj_kernel(h_ref, mods_ref, w_ref, qa_ref, qb_ref):
    u = _modulate(h_ref[...], mods_ref[0], 1).astype(jnp.bfloat16)
    scale = HEAD_DIM ** -0.5
    for grp, out_ref in enumerate((qa_ref, qb_ref)):
        for part in range(3):
            c0 = (3 * grp + part) * D_GROUP
            y = jnp.dot(u, w_ref[:, c0:c0 + D_GROUP], preferred_element_type=jnp.float32)
            if part == 0:
                y = y * scale
            out_ref[:, part * D_GROUP:(part + 1) * D_GROUP] = y.astype(out_ref.dtype)


def _proj(h, mods, w_in, seq):
    n, d = h.shape
    tiles_per_batch = seq // TM_PROJ
    return pl.pallas_call(
        _proj_kernel,
        out_shape=(jax.ShapeDtypeStruct((n, 3 * D_GROUP), jnp.float32),
                   jax.ShapeDtypeStruct((n, 3 * D_GROUP), jnp.bfloat16)),
        grid=(n // TM_PROJ,),
        in_specs=[pl.BlockSpec((TM_PROJ, d), lambda i: (i, 0)),
                  pl.BlockSpec((1, N_MOD, d), lambda i: (i // tiles_per_batch, 0, 0)),
                  pl.BlockSpec((d, 6 * D_GROUP), lambda i: (0, 0))],
        out_specs=(pl.BlockSpec((TM_PROJ, 3 * D_GROUP), lambda i: (i, 0)),
                   pl.BlockSpec((TM_PROJ, 3 * D_GROUP), lambda i: (i, 0))),
        compiler_params=pltpu.CompilerParams(dimension_semantics=("arbitrary",),
                                             vmem_limit_bytes=VMEM_LIMIT),
        name="proj",
    )(h, mods, w_in)


def _head_masks():
    lane = lax.broadcasted_iota(jnp.int32, (1, LANES), 1)
    return lane < HEAD_DIM, lane >= HEAD_DIM


def _two_head_blocks(items):
    masks = _head_masks()
    scores = []
    for q, k, _, _ in items:
        zero = jnp.zeros((), q.dtype)
        scores.append([lax.dot_general(jnp.where(masks[e], q, zero), k, (((1,), (1,)), ((), ())),
                                       preferred_element_type=jnp.float32) for e in range(2)])
    probs, stats = [], []
    for (_, _, v, bias_fn), s2 in zip(items, scores):
        ms, ls, ps = [], [], []
        for e in range(2):
            s = bias_fn(e, s2[e])
            m = jnp.max(s, axis=-1, keepdims=True)
            p = jnp.exp(s - m)
            ms.append(m)
            ls.append(jnp.sum(p, axis=-1, keepdims=True))
            ps.append(p.astype(v.dtype))
        probs.append(ps)
        stats.append((ms, ls))
    out = []
    for (_, _, v, _), ps, (ms, ls) in zip(items, probs, stats):
        zero = jnp.zeros((), v.dtype)
        acc = None
        for e in range(2):
            pv = jnp.dot(ps[e], jnp.where(masks[e], v, zero), preferred_element_type=jnp.float32)
            acc = pv if acc is None else acc + pv
        out.append((ms, ls, acc))
    return out


def _dil_kernel(slopes_ref, q_ref, k_ref, v_ref, o_ref, m_sc, l_sc, acc_sc):
    hp = pl.program_id(1)
    sb = pl.program_id(2)
    seq = q_ref.shape[1]
    h0, _ = _head_masks()
    slope = (slopes_ref[2 * hp], slopes_ref[2 * hp + 1])
    row = lax.broadcasted_iota(jnp.int32, (DIL_QB, DIL_KB), 0)
    col = lax.broadcasted_iota(jnp.int32, (DIL_QB, DIL_KB), 1)

    def rows(start, size, d):
        return pl.ds(start, size) if d == 1 else pl.ds(start, size, stride=d)

    for _, d in DIL_PATTERNS:
        sub_len = seq // d
        sub_sb = DIL_SB // d
        blocks_per_sub = sub_sb // DIL_QB

        def body(t, carry, d=d, sub_len=sub_len, sub_sb=sub_sb, blocks_per_sub=blocks_per_sub):
            items, slots = [], []
            for g in range(DIL_GROUP):
                i = t * DIL_GROUP + g
                r = i // blocks_per_sub
                jq_rel = (i % blocks_per_sub) * DIL_QB
                jq0 = sb * sub_sb + jq_rel
                ks = jnp.clip(jq0 - DIL_RADIUS, 0, sub_len - DIL_KB)
                q = q_ref[0, rows(r + d * jq0, DIL_QB, d), :].astype(jnp.bfloat16)
                k = k_ref[0, rows(r + d * ks, DIL_KB, d), :].astype(jnp.bfloat16)
                v = v_ref[0, rows(r + d * ks, DIL_KB, d), :].astype(jnp.bfloat16)
                absrel = jnp.abs(col - row + (ks - jq0))
                valid = absrel <= DIL_RADIUS
                dist = (d * absrel).astype(jnp.float32)

                def bias_fn(e, s, valid=valid, dist=dist):
                    return jnp.where(valid, s - slope[e] * dist, NEG_INF)

                items.append((q, k, v, bias_fn))
                slots.append(rows(r + d * jq_rel, DIL_QB, d))

            for st, (ms, ls, acc) in zip(slots, _two_head_blocks(items)):
                m_blk = jnp.where(h0, ms[0], ms[1])
                l_blk = jnp.where(h0, ls[0], ls[1])
                if d == 1:
                    m_sc[st, :] = m_blk
                    l_sc[st, :] = l_blk
                    acc_sc[st, :] = acc
                else:
                    m_old = m_sc[st, :]
                    m_new = jnp.maximum(m_old, m_blk)
                    a_old = jnp.exp(m_old - m_new)
                    a_blk = jnp.exp(m_blk - m_new)
                    m_sc[st, :] = m_new
                    l_sc[st, :] = a_old * l_sc[st, :] + a_blk * l_blk
                    acc_sc[st, :] = a_old * acc_sc[st, :] + a_blk * acc
            return carry

        lax.fori_loop(0, DIL_SB // DIL_QB // DIL_GROUP, body, 0)

    o_ref[0] = (acc_sc[...] / l_sc[...]).astype(o_ref.dtype)


def _dilated(qkv_a, slopes):
    B, S, _ = qkv_a.shape
    n_hp = D_GROUP // LANES
    blk = lambda part: pl.BlockSpec((1, S, LANES), lambda b, hp, sb: (b, 0, part * n_hp + hp))
    return pl.pallas_call(
        _dil_kernel,
        out_shape=jax.ShapeDtypeStruct((B, S, D_GROUP), jnp.bfloat16),
        grid=(B, n_hp, S // DIL_SB),
        in_specs=[pl.BlockSpec(memory_space=pltpu.SMEM), blk(0), blk(1), blk(2)],
        out_specs=pl.BlockSpec((1, DIL_SB, LANES), lambda b, hp, sb: (b, sb, hp)),
        scratch_shapes=[pltpu.VMEM((DIL_SB, LANES), jnp.float32)] * 3,
        compiler_params=pltpu.CompilerParams(
            dimension_semantics=("arbitrary", "arbitrary", "arbitrary"),
            vmem_limit_bytes=VMEM_LIMIT),
        name="dilated",
    )(slopes, qkv_a, qkv_a, qkv_a)


def _na_kernel(q_ref, k_ref, v_ref, bias_ref, o_ref):
    n_rows = q_ref.shape[1] // GRID_W
    h0, _ = _head_masks()
    win = NA_KH * GRID_W

    def body(t, carry):
        items, q0s = [], []
        for g in range(NA_GROUP):
            r = t * NA_GROUP + g
            rs = jnp.clip(r - NA_KH // 2, 0, n_rows - NA_KH)
            delta = r - rs
            q0 = pl.multiple_of(r * GRID_W, GRID_W)
            k0 = pl.multiple_of(rs * GRID_W, GRID_W)
            q = q_ref[0, pl.ds(q0, GRID_W), :]
            k = k_ref[0, pl.ds(k0, win), :]
            v = v_ref[0, pl.ds(k0, win), :]

            def bias_fn(e, s, delta=delta):
                return s + bias_ref[0, e, delta]

            items.append((q, k, v, bias_fn))
            q0s.append(q0)

        for q0, (_, ls, acc) in zip(q0s, _two_head_blocks(items)):
            o_ref[0, pl.ds(q0, GRID_W), :] = (acc / jnp.where(h0, ls[0], ls[1])).astype(o_ref.dtype)
        return carry

    lax.fori_loop(0, n_rows // NA_GROUP, body, 0)


def _na_bias_table(rpb):
    H, n_dr, n_dc = rpb.shape
    c = np.arange(GRID_W)[:, None]
    kc = np.arange(GRID_W)[None, :]
    col_start = np.clip(c - NA_KW // 2, 0, GRID_W - NA_KW)
    valid = (kc >= col_start) & (kc < col_start + NA_KW)
    period = n_dc + GRID_W
    z = jnp.pad(rpb, ((0, 0), (0, 0), (0, period - n_dc)))
    skew = jnp.tile(z, (1, 1, GRID_W))[:, :, :GRID_W * (period - 1)]
    skew = skew.reshape(H, n_dr, GRID_W, period - 1)
    toep = skew[:, :, :, NA_KW - 1:NA_KW - 1 + GRID_W]
    tab = jnp.stack([toep[:, NA_KH - 1 - dl:2 * NA_KH - 1 - dl] for dl in range(NA_KH)], axis=1)
    tab = jnp.transpose(tab, (0, 1, 3, 2, 4))
    tab = jnp.where(valid[None, None, :, None, :], tab, NEG_INF)
    return tab.reshape(H // 2, 2, NA_KH, GRID_W, NA_KH * GRID_W)


def _neighbourhood(qkv_b, bias_tab):
    B, S, _ = qkv_b.shape
    n_hp = D_GROUP // LANES
    blk = lambda part: pl.BlockSpec((1, S, LANES), lambda b, hp: (b, 0, part * n_hp + hp))
    return pl.pallas_call(
        _na_kernel,
        out_shape=jax.ShapeDtypeStruct((B, S, D_GROUP), jnp.bfloat16),
        grid=(B, n_hp),
        in_specs=[blk(0), blk(1), blk(2),
                  pl.BlockSpec((1,) + bias_tab.shape[1:], lambda b, hp: (hp, 0, 0, 0, 0))],
        out_specs=pl.BlockSpec((1, S, LANES), lambda b, hp: (b, 0, hp)),
        compiler_params=pltpu.CompilerParams(dimension_semantics=("arbitrary", "arbitrary"),
                                             vmem_limit_bytes=VMEM_LIMIT),
        name="nbr",
    )(qkv_b, qkv_b, qkv_b, bias_tab)


def _out_kernel(h_ref, ya_ref, yb_ref, mods_ref, w_ref, lng_ref, lnb_ref, o_ref):
    y = jnp.dot(ya_ref[...], w_ref[:D_GROUP, :], preferred_element_type=jnp.float32)
    y = y + jnp.dot(yb_ref[...], w_ref[D_GROUP:, :], preferred_element_type=jnp.float32)
    o_ref[...] = _residual_norm(h_ref[...], y, mods_ref[0], lng_ref[...], lnb_ref[...], 1, 1.0)


def _out_proj(h, ya, yb, mods, w_out, lng, lnb, seq):
    n, d = h.shape
    tiles_per_batch = seq // TM_OUT
    return pl.pallas_call(
        _out_kernel,
        out_shape=jax.ShapeDtypeStruct((n, d), jnp.float32),
        grid=(n // TM_OUT,),
        in_specs=[pl.BlockSpec((TM_OUT, d), lambda i: (i, 0)),
                  pl.BlockSpec((TM_OUT, D_GROUP), lambda i: (i, 0)),
                  pl.BlockSpec((TM_OUT, D_GROUP), lambda i: (i, 0)),
                  pl.BlockSpec((1, N_MOD, d), lambda i: (i // tiles_per_batch, 0, 0)),
                  pl.BlockSpec((2 * D_GROUP, d), lambda i: (0, 0)),
                  pl.BlockSpec((3, d), lambda i: (0, 0)),
                  pl.BlockSpec((3, d), lambda i: (0, 0))],
        out_specs=pl.BlockSpec((TM_OUT, d), lambda i: (i, 0)),
        compiler_params=pltpu.CompilerParams(dimension_semantics=("arbitrary",),
                                             vmem_limit_bytes=VMEM_LIMIT),
        name="out_proj",
    )(h, ya, yb, mods, w_out, lng, lnb)


def kernel(x, c, w_ada, b_ada, ffn1_w_gate, ffn1_w_up, ffn1_w_down, w_in, rpb, w_out,
           ffn2_w_gate, ffn2_w_up, ffn2_w_down, ln_g, ln_b):
    B, S, D = x.shape
    bf = jnp.bfloat16
    slopes = jnp.asarray(2.0 ** (-8.0 * np.arange(1, N_HEADS + 1) / N_HEADS), jnp.float32)
    h = x.reshape(B * S, D)
    for layer in range(DEPTH):
        mods = _mods(c, w_ada[layer], b_ada[layer]).reshape(B, N_MOD, D)
        lng, lnb = ln_g[layer], ln_b[layer]
        h = _ffn(h, mods, ffn1_w_gate[layer].astype(bf), ffn1_w_up[layer].astype(bf),
                 ffn1_w_down[layer].astype(bf), lng, lnb, 0, S)
        qkv_a, qkv_b = _proj(h, mods, w_in[layer].astype(bf), S)
        ya = _dilated(qkv_a.reshape(B, S, 3 * D_GROUP), slopes)
        yb = _neighbourhood(qkv_b.reshape(B, S, 3 * D_GROUP), _na_bias_table(rpb[layer]))
        h = _out_proj(h, ya.reshape(B * S, D_GROUP), yb.reshape(B * S, D_GROUP), mods,
                      w_out[layer].astype(bf), lng, lnb, S)
        h = _ffn(h, mods, ffn2_w_gate[layer].astype(bf), ffn2_w_up[layer].astype(bf),
                 ffn2_w_down[layer].astype(bf), lng, lnb, 2, S)
    return h.reshape(B, S, D)
```

```python
import functools

import numpy as np
import jax
import jax.numpy as jnp
from jax import lax
from jax.experimental import pallas as pl
from jax.experimental.pallas import tpu as pltpu

D_MODEL = 1024
HEAD_DIM = 64
N_HEADS = 8
D_GROUP = N_HEADS * HEAD_DIM
DIL_PATTERNS = ((128, 1), (512, 4), (2048, 16))
DIL_RADIUS = 64
GRID_W = 64
NA_KH = 8
NA_KW = 16
D_FF = 2816
N_MOD = 9
DEPTH = 1
DEEPNORM_ALPHA = (2.0 * DEPTH) ** 0.25
LN_EPS = 1e-5
NEG_INF = -1e30
LOG2E = 1.4426950408889634

LANES = 128
VMEM_LIMIT = 56 * 1024 * 1024

TM_FFN = 512
FFN_SUBTILES = 1
TF_FFN = 256
TM_PROJ = 512
TM_OUT = 512
TN_MODS = 1024
DIL_SB = 2048
DIL_QB = 128
DIL_KB = 256
DIL_KEY_OFFSETS = (-DIL_RADIUS, 0, DIL_QB - DIL_KB)
DIL_GROUP = 4
NA_GROUP = 4


def _layer_norm(x):
    mu = jnp.mean(x, axis=-1, keepdims=True)
    xc = x - mu
    var = jnp.mean(xc * xc, axis=-1, keepdims=True)
    return xc * lax.rsqrt(var + LN_EPS)


def _modulate(h, mods, j):
    shift = mods[3 * j:3 * j + 1]
    scale = mods[3 * j + 1:3 * j + 2]
    return _layer_norm(h) * (1.0 + scale) + shift


def _residual_norm(h, branch, mods, lng, lnb, j, resid_w):
    gate = mods[3 * j + 2:3 * j + 3]
    z = DEEPNORM_ALPHA * h + resid_w * gate * branch
    return _layer_norm(z) * lng[j:j + 1] + lnb[j:j + 1]


def _mods_kernel(ct_ref, w_ref, b_ref, o_ref):
    ct = ct_ref[...]
    a = ct * jax.nn.sigmoid(ct)
    w = w_ref[...]
    rows = [jnp.sum(w * a[:, b:b + 1], axis=0, keepdims=True) for b in range(ct.shape[1])]
    o_ref[...] = jnp.concatenate(rows, axis=0) + b_ref[...]


def _mods(c, w_ada, b_ada):
    B, D = c.shape
    n = w_ada.shape[1]
    return pl.pallas_call(
        _mods_kernel,
        out_shape=jax.ShapeDtypeStruct((B, n), jnp.float32),
        grid=(n // TN_MODS,),
        in_specs=[pl.BlockSpec((D, B), lambda i: (0, 0)),
                  pl.BlockSpec((D, TN_MODS), lambda i: (0, i)),
                  pl.BlockSpec((1, TN_MODS), lambda i: (0, i))],
        out_specs=pl.BlockSpec((B, TN_MODS), lambda i: (0, i)),
        compiler_params=pltpu.CompilerParams(dimension_semantics=("arbitrary",),
                                             vmem_limit_bytes=VMEM_LIMIT),
        name="mods",
    )(c.T, w_ada, b_ada.reshape(1, n))


def _ffn_kernel(j, h_ref, mods_ref, wg_ref, wu_ref, wd_ref, lng_ref, lnb_ref, o_ref):
    sub = TM_FFN // FFN_SUBTILES
    rows = [slice(t * sub, (t + 1) * sub) for t in range(FFN_SUBTILES)]
    us = [_modulate(h_ref[r, :], mods_ref[0], j).astype(jnp.bfloat16) for r in rows]
    for r, u in zip(rows, us):
        acc = None
        for f in range(D_FF // TF_FFN):
            cols = slice(f * TF_FFN, (f + 1) * TF_FFN)
            g = jnp.dot(u, wg_ref[:, cols], preferred_element_type=jnp.float32)
            up = jnp.dot(u, wu_ref[:, cols], preferred_element_type=jnp.float32)
            a = (g * jax.nn.sigmoid(g) * up).astype(jnp.bfloat16)
            y = jnp.dot(a, wd_ref[cols, :], preferred_element_type=jnp.float32)
            acc = y if acc is None else acc + y
        o_ref[r, :] = _residual_norm(h_ref[r, :], acc, mods_ref[0], lng_ref[...], lnb_ref[...], j, 0.5)


def _ffn(h, mods, wg, wu, wd, lng, lnb, j, seq):
    n, d = h.shape
    tiles_per_batch = seq // TM_FFN
    resident = pl.Buffered(1)
    return pl.pallas_call(
        functools.partial(_ffn_kernel, j),
        out_shape=jax.ShapeDtypeStruct((n, d), jnp.float32),
        grid=(n // TM_FFN,),
        in_specs=[pl.BlockSpec((TM_FFN, d), lambda i: (i, 0)),
                  pl.BlockSpec((1, N_MOD, d), lambda i: (i // tiles_per_batch, 0, 0)),
                  pl.BlockSpec((d, D_FF), lambda i: (0, 0), pipeline_mode=resident),
                  pl.BlockSpec((d, D_FF), lambda i: (0, 0), pipeline_mode=resident),
                  pl.BlockSpec((D_FF, d), lambda i: (0, 0), pipeline_mode=resident),
                  pl.BlockSpec((3, d), lambda i: (0, 0)),
                  pl.BlockSpec((3, d), lambda i: (0, 0))],
        out_specs=pl.BlockSpec((TM_FFN, d), lambda i: (i, 0)),
        compiler_params=pltpu.CompilerParams(dimension_semantics=("arbitrary",),
                                             vmem_limit_bytes=VMEM_LIMIT),
        name=f"ffn{j}",
    )(h, mods, wg, wu, wd, lng, lnb)


def _proj_kernel(h_ref, mods_ref, w_ref, qa_ref, qb_ref):
    u = _modulate(h_ref[...], mods_ref[0], 1).astype(jnp.bfloat16)
    scale = HEAD_DIM ** -0.5 * LOG2E
    for grp, out_ref in enumerate((qa_ref, qb_ref)):
        for part in range(3):
            c0 = (3 * grp + part) * D_GROUP
            y = jnp.dot(u, w_ref[:, c0:c0 + D_GROUP], preferred_element_type=jnp.float32)
            if part == 0:
                y = y * scale
            out_ref[:, part * D_GROUP:(part + 1) * D_GROUP] = y.astype(out_ref.dtype)


def _proj(h, mods, w_in, seq):
    n, d = h.shape
    tiles_per_batch = seq // TM_PROJ
    return pl.pallas_call(
        _proj_kernel,
        out_shape=(jax.ShapeDtypeStruct((n, 3 * D_GROUP), jnp.float32),
                   jax.ShapeDtypeStruct((n, 3 * D_GROUP), jnp.bfloat16)),
        grid=(n // TM_PROJ,),
        in_specs=[pl.BlockSpec((TM_PROJ, d), lambda i: (i, 0)),
                  pl.BlockSpec((1, N_MOD, d), lambda i: (i // tiles_per_batch, 0, 0)),
                  pl.BlockSpec((d, 6 * D_GROUP), lambda i: (0, 0))],
        out_specs=(pl.BlockSpec((TM_PROJ, 3 * D_GROUP), lambda i: (i, 0)),
                   pl.BlockSpec((TM_PROJ, 3 * D_GROUP), lambda i: (i, 0))),
        compiler_params=pltpu.CompilerParams(dimension_semantics=("arbitrary",),
                                             vmem_limit_bytes=VMEM_LIMIT),
        name="proj",
    )(h, mods, w_in)


def _head_masks():
    lane = lax.broadcasted_iota(jnp.int32, (1, LANES), 1)
    return lane < HEAD_DIM, lane >= HEAD_DIM


def _two_head_blocks(items):
    h0, h1 = _head_masks()
    scores = []
    for q, k, _, _ in items:
        zero = jnp.zeros((), q.dtype)
        q2 = jnp.concatenate([jnp.where(h0, q, zero), jnp.where(h1, q, zero)], axis=0)
        scores.append(lax.dot_general(q2, k, (((1,), (1,)), ((), ())),
                                      preferred_element_type=jnp.float32))
    probs = []
    for (_, _, v, bias_fn), s in zip(items, scores):
        s = bias_fn(s)
        m = jnp.max(s, axis=-1, keepdims=True)
        p = jnp.exp2(s - m)
        probs.append((m, jnp.sum(p, axis=-1, keepdims=True), p.astype(v.dtype)))
    out = []
    for (q, _, v, _), (m, l, p) in zip(items, probs):
        tq = q.shape[0]
        pv = jnp.dot(p, v, preferred_element_type=jnp.float32)
        out.append((jnp.where(h0, m[:tq], m[tq:]), jnp.where(h0, l[:tq], l[tq:]),
                    jnp.where(h0, pv[:tq], pv[tq:])))
    return out


def _dil_kernel(slopes_ref, q_ref, k_ref, v_ref, o_ref, m_sc, l_sc, acc_sc, bias_sc):
    hp = pl.program_id(1)
    sb = pl.program_id(2)
    seq = q_ref.shape[1]

    def rows(start, size, d):
        return pl.ds(start, size) if d == 1 else pl.ds(start, size, stride=d)

    @pl.when(sb == 0)
    def _():
        row = lax.broadcasted_iota(jnp.int32, (DIL_QB, DIL_KB), 0)
        col = lax.broadcasted_iota(jnp.int32, (DIL_QB, DIL_KB), 1)
        for p_idx, (_, d) in enumerate(DIL_PATTERNS):
            for var, off in enumerate(DIL_KEY_OFFSETS):
                absrel = jnp.abs(col - row + off)
                valid = absrel <= DIL_RADIUS
                dist = (d * absrel).astype(jnp.float32)
                tabs = [jnp.where(valid, -(slopes_ref[2 * hp + e] * LOG2E) * dist, NEG_INF)
                        for e in range(2)]
                bias_sc[p_idx, var] = jnp.concatenate(tabs, axis=0)

    for p_idx, (_, d) in enumerate(DIL_PATTERNS):
        sub_len = seq // d
        sub_sb = DIL_SB // d
        blocks_per_sub = sub_sb // DIL_QB

        def body(t, carry, p_idx=p_idx, d=d, sub_len=sub_len, sub_sb=sub_sb,
                 blocks_per_sub=blocks_per_sub):
            items, slots = [], []
            for g in range(DIL_GROUP):
                i = t * DIL_GROUP + g
                r = i // blocks_per_sub
                jq_rel = (i % blocks_per_sub) * DIL_QB
                jq0 = sb * sub_sb + jq_rel
                ks = jnp.clip(jq0 - DIL_RADIUS, 0, sub_len - DIL_KB)
                q = q_ref[0, rows(r + d * jq0, DIL_QB, d), :].astype(jnp.bfloat16)
                k = k_ref[0, rows(r + d * ks, DIL_KB, d), :].astype(jnp.bfloat16)
                v = v_ref[0, rows(r + d * ks, DIL_KB, d), :].astype(jnp.bfloat16)
                var = jnp.where(jq0 == 0, 1, jnp.where(jq0 == sub_len - DIL_QB, 2, 0))

                def bias_fn(s, var=var):
                    return s + bias_sc[p_idx, var]

                items.append((q, k, v, bias_fn))
                slots.append(rows(r + d * jq_rel, DIL_QB, d))

            for st, (m_blk, l_blk, acc) in zip(slots, _two_head_blocks(items)):
                if d == 1:
                    m_sc[st, :] = m_blk
                    l_sc[st, :] = l_blk
                    acc_sc[st, :] = acc
                else:
                    m_old = m_sc[st, :]
                    m_new = jnp.maximum(m_old, m_blk)
                    a_old = jnp.exp2(m_old - m_new)
                    a_blk = jnp.exp2(m_blk - m_new)
                    m_sc[st, :] = m_new
                    l_sc[st, :] = a_old * l_sc[st, :] + a_blk * l_blk
                    acc_sc[st, :] = a_old * acc_sc[st, :] + a_blk * acc
            return carry

        lax.fori_loop(0, DIL_SB // DIL_QB // DIL_GROUP, body, 0)

    o_ref[0] = (acc_sc[...] / l_sc[...]).astype(o_ref.dtype)


def _dilated(qkv_a, slopes):
    B, S, _ = qkv_a.shape
    n_hp = D_GROUP // LANES
    blk = lambda part: pl.BlockSpec((1, S, LANES), lambda b, hp, sb: (b, 0, part * n_hp + hp))
    return pl.pallas_call(
        _dil_kernel,
        out_shape=jax.ShapeDtypeStruct((B, S, D_GROUP), jnp.bfloat16),
        grid=(B, n_hp, S // DIL_SB),
        in_specs=[pl.BlockSpec(memory_space=pltpu.SMEM), blk(0), blk(1), blk(2)],
        out_specs=pl.BlockSpec((1, DIL_SB, LANES), lambda b, hp, sb: (b, sb, hp)),
        scratch_shapes=[pltpu.VMEM((DIL_SB, LANES), jnp.float32)] * 3 + [
            pltpu.VMEM((len(DIL_PATTERNS), len(DIL_KEY_OFFSETS), 2 * DIL_QB, DIL_KB), jnp.float32)],
        compiler_params=pltpu.CompilerParams(
            dimension_semantics=("arbitrary", "arbitrary", "arbitrary"),
            vmem_limit_bytes=VMEM_LIMIT),
        name="dilated",
    )(slopes, qkv_a, qkv_a, qkv_a)


def _na_kernel(q_ref, k_ref, v_ref, bias_ref, o_ref):
    n_rows = q_ref.shape[1] // GRID_W
    win = NA_KH * GRID_W

    def body(t, carry):
        items, q0s = [], []
        for g in range(NA_GROUP):
            r = t * NA_GROUP + g
            rs = jnp.clip(r - NA_KH // 2, 0, n_rows - NA_KH)
            delta = r - rs
            q0 = pl.multiple_of(r * GRID_W, GRID_W)
            k0 = pl.multiple_of(rs * GRID_W, GRID_W)
            q = q_ref[0, pl.ds(q0, GRID_W), :]
            k = k_ref[0, pl.ds(k0, win), :]
            v = v_ref[0, pl.ds(k0, win), :]

            def bias_fn(s, delta=delta):
                return s + bias_ref[0, delta]

            items.append((q, k, v, bias_fn))
            q0s.append(q0)

        for q0, (_, l, acc) in zip(q0s, _two_head_blocks(items)):
            o_ref[0, pl.ds(q0, GRID_W), :] = (acc / l).astype(o_ref.dtype)
        return carry

    lax.fori_loop(0, n_rows // NA_GROUP, body, 0)


def _na_bias_table(rpb):
    H, n_dr, n_dc = rpb.shape
    c = np.arange(GRID_W)[:, None]
    kc = np.arange(GRID_W)[None, :]
    col_start = np.clip(c - NA_KW // 2, 0, GRID_W - NA_KW)
    valid = (kc >= col_start) & (kc < col_start + NA_KW)
    period = n_dc + GRID_W
    z = jnp.pad(rpb, ((0, 0), (0, 0), (0, period - n_dc)))
    skew = jnp.tile(z, (1, 1, GRID_W))[:, :, :GRID_W * (period - 1)]
    skew = skew.reshape(H, n_dr, GRID_W, period - 1)
    toep = skew[:, :, :, NA_KW - 1:NA_KW - 1 + GRID_W]
    tab = jnp.stack([toep[:, NA_KH - 1 - dl:2 * NA_KH - 1 - dl] for dl in range(NA_KH)], axis=1)
    tab = jnp.transpose(tab, (0, 1, 3, 2, 4))
    tab = jnp.where(valid[None, None, :, None, :], tab * LOG2E, NEG_INF)
    tab = tab.reshape(H // 2, 2, NA_KH, GRID_W, NA_KH * GRID_W)
    return jnp.transpose(tab, (0, 2, 1, 3, 4)).reshape(H // 2, NA_KH, 2 * GRID_W, NA_KH * GRID_W)


def _neighbourhood(qkv_b, bias_tab):
    B, S, _ = qkv_b.shape
    n_hp = D_GROUP // LANES
    blk = lambda part: pl.BlockSpec((1, S, LANES), lambda b, hp: (b, 0, part * n_hp + hp))
    return pl.pallas_call(
        _na_kernel,
        out_shape=jax.ShapeDtypeStruct((B, S, D_GROUP), jnp.bfloat16),
        grid=(B, n_hp),
        in_specs=[blk(0), blk(1), blk(2),
                  pl.BlockSpec((1,) + bias_tab.shape[1:], lambda b, hp: (hp, 0, 0, 0))],
        out_specs=pl.BlockSpec((1, S, LANES), lambda b, hp: (b, 0, hp)),
        compiler_params=pltpu.CompilerParams(dimension_semantics=("arbitrary", "arbitrary"),
                                             vmem_limit_bytes=VMEM_LIMIT),
        name="nbr",
    )(qkv_b, qkv_b, qkv_b, bias_tab)


def _out_kernel(h_ref, ya_ref, yb_ref, mods_ref, w_ref, lng_ref, lnb_ref, o_ref):
    y = jnp.dot(ya_ref[...], w_ref[:D_GROUP, :], preferred_element_type=jnp.float32)
    y = y + jnp.dot(yb_ref[...], w_ref[D_GROUP:, :], preferred_element_type=jnp.float32)
    o_ref[...] = _residual_norm(h_ref[...], y, mods_ref[0], lng_ref[...], lnb_ref[...], 1, 1.0)


def _out_proj(h, ya, yb, mods, w_out, lng, lnb, seq):
    n, d = h.shape
    tiles_per_batch = seq // TM_OUT
    return pl.pallas_call(
        _out_kernel,
        out_shape=jax.ShapeDtypeStruct((n, d), jnp.float32),
        grid=(n // TM_OUT,),
        in_specs=[pl.BlockSpec((TM_OUT, d), lambda i: (i, 0)),
                  pl.BlockSpec((TM_OUT, D_GROUP), lambda i: (i, 0)),
                  pl.BlockSpec((TM_OUT, D_GROUP), lambda i: (i, 0)),
                  pl.BlockSpec((1, N_MOD, d), lambda i: (i // tiles_per_batch, 0, 0)),
                  pl.BlockSpec((2 * D_GROUP, d), lambda i: (0, 0)),
                  pl.BlockSpec((3, d), lambda i: (0, 0)),
                  pl.BlockSpec((3, d), lambda i: (0, 0))],
        out_specs=pl.BlockSpec((TM_OUT, d), lambda i: (i, 0)),
        compiler_params=pltpu.CompilerParams(dimension_semantics=("arbitrary",),
                                             vmem_limit_bytes=VMEM_LIMIT),
        name="out_proj",
    )(h, ya, yb, mods, w_out, lng, lnb)


def kernel(x, c, w_ada, b_ada, ffn1_w_gate, ffn1_w_up, ffn1_w_down, w_in, rpb, w_out,
           ffn2_w_gate, ffn2_w_up, ffn2_w_down, ln_g, ln_b):
    B, S, D = x.shape
    bf = jnp.bfloat16
    slopes = jnp.asarray(2.0 ** (-8.0 * np.arange(1, N_HEADS + 1) / N_HEADS), jnp.float32)
    h = x.reshape(B * S, D)
    for layer in range(DEPTH):
        mods = _mods(c, w_ada[layer], b_ada[layer]).reshape(B, N_MOD, D)
        lng, lnb = ln_g[layer], ln_b[layer]
        h = _ffn(h, mods, ffn1_w_gate[layer].astype(bf), ffn1_w_up[layer].astype(bf),
                 ffn1_w_down[layer].astype(bf), lng, lnb, 0, S)
        qkv_a, qkv_b = _proj(h, mods, w_in[layer].astype(bf), S)
        ya = _dilated(qkv_a.reshape(B, S, 3 * D_GROUP), slopes)
        yb = _neighbourhood(qkv_b.reshape(B, S, 3 * D_GROUP), _na_bias_table(rpb[layer]))
        h = _out_proj(h, ya.reshape(B * S, D_GROUP), yb.reshape(B * S, D_GROUP), mods,
                      w_out[layer].astype(bf), lng, lnb, S)
        h = _ffn(h, mods, ffn2_w_gate[layer].astype(bf), ffn2_w_up[layer].astype(bf),
                 ffn2_w_down[layer].astype(bf), lng, lnb, 2, S)
    return h.reshape(B, S, D)
```

```python
import functools

import numpy as np
import jax
import jax.numpy as jnp
from jax import lax
from jax.experimental import pallas as pl
from jax.experimental.pallas import tpu as pltpu

D_MODEL = 1024
HEAD_DIM = 64
N_HEADS = 8
D_GROUP = N_HEADS * HEAD_DIM
DIL_PATTERNS = ((128, 1), (512, 4), (2048, 16))
DIL_RADIUS = 64
GRID_W = 64
NA_KH = 8
NA_KW = 16
D_FF = 2816
N_MOD = 9
DEPTH = 1
DEEPNORM_ALPHA = (2.0 * DEPTH) ** 0.25
LN_EPS = 1e-5
NEG_INF = -1e30
LOG2E = 1.4426950408889634

LANES = 128
VMEM_LIMIT = 56 * 1024 * 1024

TM_FFN = 512
FFN_SUBTILES = 1
TF_FFN = 256
TM_PROJ = 512
TM_OUT = 512
TN_MODS = 1024
DIL_SB = 2048
DIL_QB = 128
DIL_KB = 256
DIL_KEY_OFFSETS = (-DIL_RADIUS, 0, DIL_QB - DIL_KB)
DIL_GROUP = (8, 4, 4)
NA_GROUP = 8


def _layer_norm(x):
    mu = jnp.mean(x, axis=-1, keepdims=True)
    xc = x - mu
    var = jnp.mean(xc * xc, axis=-1, keepdims=True)
    return xc * lax.rsqrt(var + LN_EPS)


def _modulate(h, mods, j):
    shift = mods[3 * j:3 * j + 1]
    scale = mods[3 * j + 1:3 * j + 2]
    return _layer_norm(h) * (1.0 + scale) + shift


def _residual_norm(h, branch, mods, lng, lnb, j, resid_w):
    gate = mods[3 * j + 2:3 * j + 3]
    z = DEEPNORM_ALPHA * h + resid_w * gate * branch
    return _layer_norm(z) * lng[j:j + 1] + lnb[j:j + 1]


def _mods_kernel(ct_ref, w_ref, b_ref, o_ref):
    ct = ct_ref[...]
    a = ct * jax.nn.sigmoid(ct)
    w = w_ref[...]
    rows = [jnp.sum(w * a[:, b:b + 1], axis=0, keepdims=True) for b in range(ct.shape[1])]
    o_ref[...] = jnp.concatenate(rows, axis=0) + b_ref[...]


def _mods(c, w_ada, b_ada):
    B, D = c.shape
    n = w_ada.shape[1]
    return pl.pallas_call(
        _mods_kernel,
        out_shape=jax.ShapeDtypeStruct((B, n), jnp.float32),
        grid=(n // TN_MODS,),
        in_specs=[pl.BlockSpec((D, B), lambda i: (0, 0)),
                  pl.BlockSpec((D, TN_MODS), lambda i: (0, i)),
                  pl.BlockSpec((1, TN_MODS), lambda i: (0, i))],
        out_specs=pl.BlockSpec((B, TN_MODS), lambda i: (0, i)),
        compiler_params=pltpu.CompilerParams(dimension_semantics=("arbitrary",),
                                             vmem_limit_bytes=VMEM_LIMIT),
        name="mods",
    )(c.T, w_ada, b_ada.reshape(1, n))


def _ffn_kernel(j, h_ref, mods_ref, wg_ref, wu_ref, wd_ref, lng_ref, lnb_ref, o_ref):
    sub = TM_FFN // FFN_SUBTILES
    rows = [slice(t * sub, (t + 1) * sub) for t in range(FFN_SUBTILES)]
    us = [_modulate(h_ref[r, :], mods_ref[0], j).astype(jnp.bfloat16) for r in rows]
    for r, u in zip(rows, us):
        acc = None
        for f in range(D_FF // TF_FFN):
            cols = slice(f * TF_FFN, (f + 1) * TF_FFN)
            g = jnp.dot(u, wg_ref[:, cols], preferred_element_type=jnp.float32)
            up = jnp.dot(u, wu_ref[:, cols], preferred_element_type=jnp.float32)
            a = (g * jax.nn.sigmoid(g) * up).astype(jnp.bfloat16)
            y = jnp.dot(a, wd_ref[cols, :], preferred_element_type=jnp.float32)
            acc = y if acc is None else acc + y
        o_ref[r, :] = _residual_norm(h_ref[r, :], acc, mods_ref[0], lng_ref[...], lnb_ref[...], j, 0.5)


def _ffn(h, mods, wg, wu, wd, lng, lnb, j, seq):
    n, d = h.shape
    tiles_per_batch = seq // TM_FFN
    resident = pl.Buffered(1)
    return pl.pallas_call(
        functools.partial(_ffn_kernel, j),
        out_shape=jax.ShapeDtypeStruct((n, d), jnp.float32),
        grid=(n // TM_FFN,),
        in_specs=[pl.BlockSpec((TM_FFN, d), lambda i: (i, 0)),
                  pl.BlockSpec((1, N_MOD, d), lambda i: (i // tiles_per_batch, 0, 0)),
                  pl.BlockSpec((d, D_FF), lambda i: (0, 0), pipeline_mode=resident),
                  pl.BlockSpec((d, D_FF), lambda i: (0, 0), pipeline_mode=resident),
                  pl.BlockSpec((D_FF, d), lambda i: (0, 0), pipeline_mode=resident),
                  pl.BlockSpec((3, d), lambda i: (0, 0)),
                  pl.BlockSpec((3, d), lambda i: (0, 0))],
        out_specs=pl.BlockSpec((TM_FFN, d), lambda i: (i, 0)),
        compiler_params=pltpu.CompilerParams(dimension_semantics=("arbitrary",),
                                             vmem_limit_bytes=VMEM_LIMIT),
        name=f"ffn{j}",
    )(h, mods, wg, wu, wd, lng, lnb)


def _proj_kernel(h_ref, mods_ref, w_ref, qa_ref, qb_ref):
    u = _modulate(h_ref[...], mods_ref[0], 1).astype(jnp.bfloat16)
    scale = HEAD_DIM ** -0.5 * LOG2E
    for grp, out_ref in enumerate((qa_ref, qb_ref)):
        for part in range(3):
            c0 = (3 * grp + part) * D_GROUP
            y = jnp.dot(u, w_ref[:, c0:c0 + D_GROUP], preferred_element_type=jnp.float32)
            if part == 0:
                y = y * scale
            out_ref[:, part * D_GROUP:(part + 1) * D_GROUP] = y.astype(out_ref.dtype)


def _proj(h, mods, w_in, seq):
    n, d = h.shape
    tiles_per_batch = seq // TM_PROJ
    return pl.pallas_call(
        _proj_kernel,
        out_shape=(jax.ShapeDtypeStruct((n, 3 * D_GROUP), jnp.float32),
                   jax.ShapeDtypeStruct((n, 3 * D_GROUP), jnp.bfloat16)),
        grid=(n // TM_PROJ,),
        in_specs=[pl.BlockSpec((TM_PROJ, d), lambda i: (i, 0)),
                  pl.BlockSpec((1, N_MOD, d), lambda i: (i // tiles_per_batch, 0, 0)),
                  pl.BlockSpec((d, 6 * D_GROUP), lambda i: (0, 0))],
        out_specs=(pl.BlockSpec((TM_PROJ, 3 * D_GROUP), lambda i: (i, 0)),
                   pl.BlockSpec((TM_PROJ, 3 * D_GROUP), lambda i: (i, 0))),
        compiler_params=pltpu.CompilerParams(dimension_semantics=("arbitrary",),
                                             vmem_limit_bytes=VMEM_LIMIT),
        name="proj",
    )(h, mods, w_in)


def _head_masks():
    lane = lax.broadcasted_iota(jnp.int32, (1, LANES), 1)
    return lane < HEAD_DIM, lane >= HEAD_DIM


def _two_head_blocks(items, mxu_row_sum):
    h0, h1 = _head_masks()
    scores = []
    for q, k, _, _ in items:
        zero = jnp.zeros((), q.dtype)
        q2 = jnp.concatenate([jnp.where(h0, q, zero), jnp.where(h1, q, zero)], axis=0)
        scores.append(lax.dot_general(q2, k, (((1,), (1,)), ((), ())),
                                      preferred_element_type=jnp.float32))
    probs = []
    for (_, _, v, bias_fn), s in zip(items, scores):
        s = bias_fn(s)
        m = jnp.max(s, axis=-1, keepdims=True)
        p = jnp.exp2(s - m)
        l = None if mxu_row_sum else jnp.sum(p, axis=-1, keepdims=True)
        probs.append((m, l, p.astype(v.dtype)))
    out = []
    for (q, _, v, _), (m, l, p) in zip(items, probs):
        tq = q.shape[0]
        if mxu_row_sum:
            v_ones = jnp.concatenate([v, jnp.ones_like(v)], axis=1)
            pv = jnp.dot(p, v_ones, preferred_element_type=jnp.float32)
            acc, l = pv[:, :LANES], pv[:, LANES:]
        else:
            acc = jnp.dot(p, v, preferred_element_type=jnp.float32)
        out.append((jnp.where(h0, m[:tq], m[tq:]), jnp.where(h0, l[:tq], l[tq:]),
                    jnp.where(h0, acc[:tq], acc[tq:])))
    return out


def _dil_kernel(slopes_ref, q_ref, k_ref, v_ref, o_ref, m_sc, l_sc, acc_sc, bias_sc):
    hp = pl.program_id(1)
    sb = pl.program_id(2)
    seq = q_ref.shape[1]

    def rows(start, size, d):
        return pl.ds(start, size) if d == 1 else pl.ds(start, size, stride=d)

    @pl.when(sb == 0)
    def _():
        row = lax.broadcasted_iota(jnp.int32, (DIL_QB, DIL_KB), 0)
        col = lax.broadcasted_iota(jnp.int32, (DIL_QB, DIL_KB), 1)
        for p_idx, (_, d) in enumerate(DIL_PATTERNS):
            for var, off in enumerate(DIL_KEY_OFFSETS):
                absrel = jnp.abs(col - row + off)
                valid = absrel <= DIL_RADIUS
                dist = (d * absrel).astype(jnp.float32)
                tabs = [jnp.where(valid, -(slopes_ref[2 * hp + e] * LOG2E) * dist, NEG_INF)
                        for e in range(2)]
                bias_sc[p_idx, var] = jnp.concatenate(tabs, axis=0)

    for p_idx, (_, d) in enumerate(DIL_PATTERNS):
        sub_len = seq // d
        sub_sb = DIL_SB // d
        blocks_per_sub = sub_sb // DIL_QB
        group = DIL_GROUP[p_idx]

        def body(t, carry, p_idx=p_idx, d=d, sub_len=sub_len, sub_sb=sub_sb,
                 blocks_per_sub=blocks_per_sub, group=group):
            items, slots = [], []
            for g in range(group):
                i = t * group + g
                r = i // blocks_per_sub
                jq_rel = (i % blocks_per_sub) * DIL_QB
                jq0 = sb * sub_sb + jq_rel
                ks = jnp.clip(jq0 - DIL_RADIUS, 0, sub_len - DIL_KB)
                q = q_ref[0, rows(r + d * jq0, DIL_QB, d), :].astype(jnp.bfloat16)
                k = k_ref[0, rows(r + d * ks, DIL_KB, d), :].astype(jnp.bfloat16)
                v = v_ref[0, rows(r + d * ks, DIL_KB, d), :].astype(jnp.bfloat16)
                var = jnp.where(jq0 == 0, 1, jnp.where(jq0 == sub_len - DIL_QB, 2, 0))

                def bias_fn(s, var=var):
                    return s + bias_sc[p_idx, var]

                items.append((q, k, v, bias_fn))
                slots.append(rows(r + d * jq_rel, DIL_QB, d))

            for st, (m_blk, l_blk, acc) in zip(slots, _two_head_blocks(items, True)):
                if d == 1:
                    m_sc[st, :] = m_blk
                    l_sc[st, :] = l_blk
                    acc_sc[st, :] = acc
                else:
                    m_old = m_sc[st, :]
                    m_new = jnp.maximum(m_old, m_blk)
                    a_old = jnp.exp2(m_old - m_new)
                    a_blk = jnp.exp2(m_blk - m_new)
                    m_sc[st, :] = m_new
                    l_sc[st, :] = a_old * l_sc[st, :] + a_blk * l_blk
                    acc_sc[st, :] = a_old * acc_sc[st, :] + a_blk * acc
            return carry

        lax.fori_loop(0, DIL_SB // DIL_QB // group, body, 0)

    o_ref[0] = (acc_sc[...] / l_sc[...]).astype(o_ref.dtype)


def _dilated(qkv_a, slopes):
    B, S, _ = qkv_a.shape
    n_hp = D_GROUP // LANES
    blk = lambda part: pl.BlockSpec((1, S, LANES), lambda b, hp, sb: (b, 0, part * n_hp + hp))
    return pl.pallas_call(
        _dil_kernel,
        out_shape=jax.ShapeDtypeStruct((B, S, D_GROUP), jnp.bfloat16),
        grid=(B, n_hp, S // DIL_SB),
        in_specs=[pl.BlockSpec(memory_space=pltpu.SMEM), blk(0), blk(1), blk(2)],
        out_specs=pl.BlockSpec((1, DIL_SB, LANES), lambda b, hp, sb: (b, sb, hp)),
        scratch_shapes=[pltpu.VMEM((DIL_SB, LANES), jnp.float32)] * 3 + [
            pltpu.VMEM((len(DIL_PATTERNS), len(DIL_KEY_OFFSETS), 2 * DIL_QB, DIL_KB), jnp.float32)],
        compiler_params=pltpu.CompilerParams(
            dimension_semantics=("arbitrary", "arbitrary", "arbitrary"),
            vmem_limit_bytes=VMEM_LIMIT),
        name="dilated",
    )(slopes, qkv_a, qkv_a, qkv_a)


def _na_kernel(q_ref, k_ref, v_ref, toep_ref, o_ref, bias_sc):
    n_rows = q_ref.shape[1] // GRID_W
    win = NA_KH * GRID_W

    h0, _ = _head_masks()
    for delta in range(NA_KH):
        for e in range(2):
            for pair in range(NA_KH // 2):
                m = 2 * pair - delta + NA_KH - 1
                tile = jnp.where(h0, toep_ref[0, e, m], toep_ref[0, e, m + 1])
                bias_sc[delta, e * GRID_W:(e + 1) * GRID_W, pair * LANES:(pair + 1) * LANES] = tile

    def body(t, carry):
        items, q0s = [], []
        for g in range(NA_GROUP):
            r = t * NA_GROUP + g
            rs = jnp.clip(r - NA_KH // 2, 0, n_rows - NA_KH)
            delta = r - rs
            q0 = pl.multiple_of(r * GRID_W, GRID_W)
            k0 = pl.multiple_of(rs * GRID_W, GRID_W)
            q = q_ref[0, pl.ds(q0, GRID_W), :]
            k = k_ref[0, pl.ds(k0, win), :]
            v = v_ref[0, pl.ds(k0, win), :]

            def bias_fn(s, delta=delta):
                return s + bias_sc[delta]

            items.append((q, k, v, bias_fn))
            q0s.append(q0)

        for q0, (_, l, acc) in zip(q0s, _two_head_blocks(items, False)):
            o_ref[0, pl.ds(q0, GRID_W), :] = (acc / l).astype(o_ref.dtype)
        return carry

    lax.fori_loop(0, n_rows // NA_GROUP, body, 0)


def _na_bias_table(rpb):
    H, n_dr, n_dc = rpb.shape
    c = np.arange(GRID_W)[:, None]
    kc = np.arange(GRID_W)[None, :]
    col_start = np.clip(c - NA_KW // 2, 0, GRID_W - NA_KW)
    valid = (kc >= col_start) & (kc < col_start + NA_KW)
    period = n_dc + GRID_W
    z = jnp.pad(rpb, ((0, 0), (0, 0), (0, period - n_dc)))
    skew = jnp.tile(z, (1, 1, GRID_W))[:, :, :GRID_W * (period - 1)]
    skew = skew.reshape(H, n_dr, GRID_W, period - 1)
    toep = skew[:, :, :, NA_KW - 1:NA_KW - 1 + GRID_W]
    toep = jnp.where(valid[None, None], toep * LOG2E, NEG_INF)
    return jnp.concatenate([toep, toep], axis=-1).reshape(H // 2, 2, n_dr, GRID_W, 2 * GRID_W)


def _neighbourhood(qkv_b, bias_tab):
    B, S, _ = qkv_b.shape
    n_hp = D_GROUP // LANES
    blk = lambda part: pl.BlockSpec((1, S, LANES), lambda b, hp: (b, 0, part * n_hp + hp))
    return pl.pallas_call(
        _na_kernel,
        out_shape=jax.ShapeDtypeStruct((B, S, D_GROUP), jnp.bfloat16),
        grid=(B, n_hp),
        in_specs=[blk(0), blk(1), blk(2),
                  pl.BlockSpec((1,) + bias_tab.shape[1:], lambda b, hp: (hp, 0, 0, 0, 0))],
        out_specs=pl.BlockSpec((1, S, LANES), lambda b, hp: (b, 0, hp)),
        scratch_shapes=[pltpu.VMEM((NA_KH, 2 * GRID_W, NA_KH * GRID_W), jnp.float32)],
        compiler_params=pltpu.CompilerParams(dimension_semantics=("arbitrary", "arbitrary"),
                                             vmem_limit_bytes=VMEM_LIMIT),
        name="nbr",
    )(qkv_b, qkv_b, qkv_b, bias_tab)


def _out_kernel(h_ref, ya_ref, yb_ref, mods_ref, w_ref, lng_ref, lnb_ref, o_ref):
    y = jnp.dot(ya_ref[...], w_ref[:D_GROUP, :], preferred_element_type=jnp.float32)
    y = y + jnp.dot(yb_ref[...], w_ref[D_GROUP:, :], preferred_element_type=jnp.float32)
    o_ref[...] = _residual_norm(h_ref[...], y, mods_ref[0], lng_ref[...], lnb_ref[...], 1, 1.0)


def _out_proj(h, ya, yb, mods, w_out, lng, lnb, seq):
    n, d = h.shape
    tiles_per_batch = seq // TM_OUT
    return pl.pallas_call(
        _out_kernel,
        out_shape=jax.ShapeDtypeStruct((n, d), jnp.float32),
        grid=(n // TM_OUT,),
        in_specs=[pl.BlockSpec((TM_OUT, d), lambda i: (i, 0)),
                  pl.BlockSpec((TM_OUT, D_GROUP), lambda i: (i, 0)),
                  pl.BlockSpec((TM_OUT, D_GROUP), lambda i: (i, 0)),
                  pl.BlockSpec((1, N_MOD, d), lambda i: (i // tiles_per_batch, 0, 0)),
                  pl.BlockSpec((2 * D_GROUP, d), lambda i: (0, 0)),
                  pl.BlockSpec((3, d), lambda i: (0, 0)),
                  pl.BlockSpec((3, d), lambda i: (0, 0))],
        out_specs=pl.BlockSpec((TM_OUT, d), lambda i: (i, 0)),
        compiler_params=pltpu.CompilerParams(dimension_semantics=("arbitrary",),
                                             vmem_limit_bytes=VMEM_LIMIT),
        name="out_proj",
    )(h, ya, yb, mods, w_out, lng, lnb)


def kernel(x, c, w_ada, b_ada, ffn1_w_gate, ffn1_w_up, ffn1_w_down, w_in, rpb, w_out,
           ffn2_w_gate, ffn2_w_up, ffn2_w_down, ln_g, ln_b):
    B, S, D = x.shape
    bf = jnp.bfloat16
    slopes = jnp.asarray(2.0 ** (-8.0 * np.arange(1, N_HEADS + 1) / N_HEADS), jnp.float32)
    h = x.reshape(B * S, D)
    for layer in range(DEPTH):
        mods = _mods(c, w_ada[layer], b_ada[layer]).reshape(B, N_MOD, D)
        lng, lnb = ln_g[layer], ln_b[layer]
        h = _ffn(h, mods, ffn1_w_gate[layer].astype(bf), ffn1_w_up[layer].astype(bf),
                 ffn1_w_down[layer].astype(bf), lng, lnb, 0, S)
        qkv_a, qkv_b = _proj(h, mods, w_in[layer].astype(bf), S)
        ya = _dilated(qkv_a.reshape(B, S, 3 * D_GROUP), slopes)
        yb = _neighbourhood(qkv_b.reshape(B, S, 3 * D_GROUP), _na_bias_table(rpb[layer]))
        h = _out_proj(h, ya.reshape(B * S, D_GROUP), yb.reshape(B * S, D_GROUP), mods,
                      w_out[layer].astype(bf), lng, lnb, S)
        h = _ffn(h, mods, ffn2_w_gate[layer].astype(bf), ffn2_w_up[layer].astype(bf),
                 ffn2_w_down[layer].astype(bf), lng, lnb, 2, S)
    return h.reshape(B, S, D)
```
